```python
import math, functools
import jax, jax.numpy as jnp
from jax import lax
import numpy as np

D_MODEL = 4096
BATCH = 4
SEQ = 2048
DEPTH = 4
DEC_BATCH = 128
DEC_SEQ = 4
PAST_LEN = 8192
PAGE_SIZE = 128

D_MIX = D_MODEL
MLA_HEADS = 16
MLA_NOPE = 128
MLA_ROPE = 64
MLA_V = 128
MLA_QK = MLA_NOPE + MLA_ROPE
Q_LORA = D_MODEL // 4
KV_LORA = 512
MLA_WIDTH = MLA_HEADS * MLA_V
POOL_WIDTH = D_MIX // 4
POOL_WINDOWS = (2, 4, 8, 16)
POOL_GROUPS = len(POOL_WINDOWS)
POOL_GDIM = POOL_WIDTH // POOL_GROUPS
POOL_BUF = max(POOL_WINDOWS) - 1
RET_WIDTH = D_MIX - MLA_WIDTH - POOL_WIDTH
RET_HEADS = 8
RET_DIM = RET_WIDTH // RET_HEADS
RET_CHUNK = 128
D_FF = 2 * D_MODEL
CONV_W = 3
PLE_DIM = 256
ROPE_THETA = 10000.0
Q_BLOCK = 128
NEG_INF = -1e30
DEEPNORM_ALPHA = (2 * DEPTH) ** 0.25
DEEPNORM_BETA = (8 * DEPTH) ** -0.25
IN_SIZES = (Q_LORA, KV_LORA, MLA_ROPE, POOL_WIDTH, RET_WIDTH, RET_WIDTH, RET_WIDTH, RET_WIDTH)
IN_COLS = sum(IN_SIZES)
IN_SPLITS = tuple(sum(IN_SIZES[:i + 1]) for i in range(len(IN_SIZES) - 1))

kernel_name = 'hymba_mla_pool_retnet_deepnorm_step'


def _layernorm(x, g, b, eps=1e-5):
    xf = x.astype(jnp.float32)
    mu = jnp.mean(xf, -1, keepdims=True)
    var = jnp.mean(jnp.square(xf - mu), -1, keepdims=True)
    return ((xf - mu) * lax.rsqrt(var + eps) * g.astype(jnp.float32) + b.astype(jnp.float32)).astype(x.dtype)


def _rmsnorm(x, g, eps=1e-6):
    xf = x.astype(jnp.float32)
    y = xf * lax.rsqrt(jnp.mean(jnp.square(xf), -1, keepdims=True) + eps)
    return (y * g.astype(jnp.float32)).astype(x.dtype)


def _group_norm(o, g, eps=1e-5):
    of = o.astype(jnp.float32)
    mu = jnp.mean(of, -1, keepdims=True)
    var = jnp.mean(jnp.square(of - mu), -1, keepdims=True)
    y = ((of - mu) * lax.rsqrt(var + eps)).reshape(o.shape[0], o.shape[1], -1)
    return (y * g.astype(jnp.float32)).astype(o.dtype)


def _rope(x, pos):
    d = x.shape[-1]
    inv = ROPE_THETA ** (-jnp.arange(0, d, 2, dtype=jnp.float32) / d)
    ang = pos.astype(jnp.float32)[:, None] * inv[None, :]
    ang = ang.reshape(ang.shape[:1] + (1,) * (x.ndim - 3) + ang.shape[1:])
    cos, sin = jnp.cos(ang), jnp.sin(ang)
    x1, x2 = jnp.split(x.astype(jnp.float32), 2, axis=-1)
    return jnp.concatenate([x1 * cos - x2 * sin, x1 * sin + x2 * cos], -1).astype(x.dtype)


def _mla_attend(q_lat, q_pe, k_c, k_pe, q_pos, k_pos):
    s = (jnp.einsum('...thc,...kc->...htk', q_lat, k_c, preferred_element_type=jnp.float32)
         + jnp.einsum('...thr,...kr->...htk', q_pe, k_pe, preferred_element_type=jnp.float32)) * (MLA_QK ** -0.5)
    s = jnp.where(k_pos[None, :] <= q_pos[:, None], s, NEG_INF)
    p = jax.nn.softmax(s, axis=-1).astype(k_c.dtype)
    return jnp.einsum('...htk,...kc->...thc', p, k_c)


def _mla_prompt(q_lat, q_pe, c_kv, k_pe, pos):
    B, S, H, C = q_lat.shape
    nb = S // Q_BLOCK
    ql = q_lat.reshape(B, nb, Q_BLOCK, H, C).swapaxes(0, 1)
    qp = q_pe.reshape(B, nb, Q_BLOCK, H, MLA_ROPE).swapaxes(0, 1)
    qpos = pos.reshape(nb, Q_BLOCK)
    o = lax.map(lambda a: _mla_attend(a[0], a[1], c_kv, k_pe, a[2], pos), (ql, qp, qpos))
    return o.swapaxes(0, 1).reshape(B, S, H, C)


def _mla_sample(q_lat, q_pe, c_kv, k_pe, pos, cache_c, cache_p, page_table):
    past = page_table.shape[1] * cache_c.shape[1]
    k_pos = jnp.arange(past + q_lat.shape[1], dtype=jnp.int32)

    def one(a):
        pt, ql, qp, cn, pn = a
        kc = jnp.concatenate([cache_c[pt].reshape(past, KV_LORA), cn.astype(cache_c.dtype)], 0)
        kp = jnp.concatenate([cache_p[pt].reshape(past, MLA_ROPE), pn.astype(cache_p.dtype)], 0)
        return _mla_attend(ql, qp, kc, kp, pos, k_pos)

    return lax.map(one, (page_table, q_lat, q_pe, c_kv, k_pe))


def _retention(q, k, v, state, chunk):
    B, L, H, d = q.shape
    n = L // chunk
    dt = q.dtype
    log_g = jnp.log1p(-jnp.exp2(-5.0 - jnp.arange(H, dtype=jnp.float32)))
    idx = jnp.arange(chunk, dtype=jnp.float32)
    diff = idx[:, None] - idx[None, :]
    d_intra = jnp.where(diff >= 0, jnp.exp(log_g[:, None, None] * jnp.maximum(diff, 0.0)), 0.0).astype(dt)
    d_q = jnp.exp(log_g[None, :] * (idx[:, None] + 1.0)).astype(dt)[None, :, :, None]
    d_k = jnp.exp(log_g[None, :] * (chunk - 1.0 - idx[:, None])).astype(dt)[None, :, :, None]
    d_c = jnp.exp(log_g * chunk).astype(dt)[None, :, None, None]
    to_chunks = lambda a: a.reshape(B, n, chunk, H, d).swapaxes(0, 1)

    def step(S, blk):
        qb, kb, vb = blk
        s = jnp.einsum('bihd,bjhd->bhij', qb, kb) * d_intra
        o = jnp.einsum('bhij,bjhe->bihe', s, vb) + jnp.einsum('bihd,bhde->bihe', qb, S) * d_q
        S = S * d_c + jnp.einsum('bjhd,bjhe->bhde', kb * d_k, vb)
        return S, o

    S, o = lax.scan(step, state.astype(dt), (to_chunks(q), to_chunks(k), to_chunks(v)))
    return o.swapaxes(0, 1).reshape(B, L, H, d), S


def _pool_mix(xp, prefix, past_valid, w_pool, pool_scale):
    B, L, _ = xp.shape
    ext = jnp.concatenate([prefix.astype(xp.dtype), xp], 1)
    cs = jnp.cumsum(ext.astype(jnp.float32), axis=1)
    cs = jnp.concatenate([jnp.zeros((B, 1, POOL_WIDTH), jnp.float32), cs], 1)
    t = jnp.arange(L, dtype=jnp.float32)
    xf = xp.astype(jnp.float32)
    outs = []
    for g, w in enumerate(POOL_WINDOWS):
        sl = slice(g * POOL_GDIM, (g + 1) * POOL_GDIM)
        win_sum = cs[:, POOL_BUF + 1:POOL_BUF + 1 + L, sl] - cs[:, POOL_BUF + 1 - w:POOL_BUF + 1 - w + L, sl]
        cnt = jnp.minimum(t + 1.0 + past_valid, float(w))
        outs.append(win_sum / cnt[None, :, None] - xf[..., sl])
    d = jnp.stack(outs, axis=2).astype(xp.dtype)
    y = jnp.einsum('blgc,gcd->blgd', d, w_pool).reshape(B, L, POOL_WIDTH) * pool_scale
    return y, ext[:, -POOL_BUF:]


def _conv_ffn(x, buf, w_up, conv_w, conv_b, w_down):
    L = x.shape[1]
    u = x @ w_up
    ext = jnp.concatenate([buf.astype(u.dtype), u], 1)
    y = conv_b
    for j in range(CONV_W):
        y = y + ext[:, j:j + L] * conv_w[j]
    a, b = jnp.split(y, 2, axis=-1)
    return (jax.nn.silu(a) * b) @ w_down, ext[:, -(CONV_W - 1):]


def _layer(x, p, pos, lw, attend, ret_state, ret_chunk, pool_prefix, past_valid, conv_buf):
    B, L, _ = x.shape
    h = x @ lw['w_in']
    c_q, c_kv, k_pe, x_pool, r_q, r_k, r_v, r_g = jnp.split(h, IN_SPLITS, axis=-1)
    q = (_rmsnorm(c_q, lw['q_norm']) @ lw['w_uq']).reshape(B, L, MLA_HEADS, MLA_QK)
    q_nope, q_pe = q[..., :MLA_NOPE], _rope(q[..., MLA_NOPE:], pos)
    c_kv = _rmsnorm(c_kv, lw['kv_norm'])
    k_pe = _rope(k_pe, pos)
    q_lat = jnp.einsum('blhd,chd->blhc', q_nope, lw['w_uk'])
    o_lat = attend(q_lat, q_pe, c_kv, k_pe)
    o_mla = jnp.einsum('blhc,chd->blhd', o_lat, lw['w_uv']).reshape(B, L, MLA_WIDTH)
    o_pool, pool_new = _pool_mix(x_pool, pool_prefix, past_valid, lw['w_pool'], lw['pool_scale'])
    rq = _rope(r_q.reshape(B, L, RET_HEADS, RET_DIM), pos)
    rk = _rope(r_k.reshape(B, L, RET_HEADS, RET_DIM), pos) * (RET_DIM ** -0.5)
    rv = r_v.reshape(B, L, RET_HEADS, RET_DIM)
    o_ret, ret_new = _retention(rq, rk, rv, ret_state, ret_chunk)
    o_ret = jax.nn.silu(r_g) * _group_norm(o_ret, lw['ret_norm'])
    mix = jnp.concatenate([o_mla, o_pool, o_ret], -1) @ lw['w_out']
    x = _layernorm(DEEPNORM_ALPHA * x + mix, lw['ln1_g'], lw['ln1_b'])
    f, conv_new = _conv_ffn(x, conv_buf, lw['w_up'], lw['conv_w'], lw['conv_b'], lw['w_down'])
    e = (p @ lw['w_pe']) * jax.nn.sigmoid(x @ lw['w_pg'])
    x = _layernorm(DEEPNORM_ALPHA * x + f + e, lw['ln2_g'], lw['ln2_b'])
    return x, (c_kv, k_pe, ret_new, pool_new, conv_new)


def setup_inputs(seed: int = 0) -> dict:
    key = jax.random.key(seed)
    ks = iter(jax.random.split(key, 40))
    f32 = jnp.float32
    n_pages = PAST_LEN // PAGE_SIZE
    n_pool = (DEC_BATCH * n_pages * 5) // 4

    def nrm(shape, scale=1.0):
        a = jax.random.normal(next(ks), shape, f32)
        return a if scale == 1.0 else a * scale

    def gain(shape):
        return 1.0 + nrm(shape, 0.02)

    inp = {}
    inp['x_prompt'] = nrm((BATCH, SEQ, D_MODEL))
    inp['x_sample'] = nrm((DEC_BATCH, DEC_SEQ, D_MODEL))
    inp['p_prompt'] = nrm((DEPTH, BATCH, SEQ, PLE_DIM))
    inp['p_sample'] = nrm((DEPTH, DEC_BATCH, DEC_SEQ, PLE_DIM))
    inp['cache_ckv'] = nrm((DEPTH, n_pool, PAGE_SIZE, KV_LORA))
    inp['cache_kpe'] = nrm((DEPTH, n_pool, PAGE_SIZE, MLA_ROPE))
    inp['state_ret'] = nrm((DEPTH, DEC_BATCH, RET_HEADS, RET_DIM, RET_DIM), 0.5)
    inp['state_pool'] = nrm((DEPTH, DEC_BATCH, POOL_BUF, POOL_WIDTH))
    inp['state_conv'] = nrm((DEPTH, DEC_BATCH, CONV_W - 1, 2 * D_FF))
    inp['page_table'] = jax.random.permutation(next(ks), n_pool)[:DEC_BATCH * n_pages].reshape(DEC_BATCH, n_pages).astype(jnp.int32)
    inp['w_in'] = nrm((DEPTH, D_MODEL, IN_COLS), D_MODEL ** -0.5)
    inp['q_norm'] = gain((DEPTH, Q_LORA))
    inp['kv_norm'] = gain((DEPTH, KV_LORA))
    inp['w_uq'] = nrm((DEPTH, Q_LORA, MLA_HEADS * MLA_QK), Q_LORA ** -0.5)
    inp['w_uk'] = nrm((DEPTH, KV_LORA, MLA_HEADS, MLA_NOPE), KV_LORA ** -0.5)
    inp['w_uv'] = nrm((DEPTH, KV_LORA, MLA_HEADS, MLA_V), KV_LORA ** -0.5)
    inp['w_pool'] = nrm((DEPTH, POOL_GROUPS, POOL_GDIM, POOL_GDIM), POOL_GDIM ** -0.5)
    inp['pool_scale'] = gain((DEPTH, POOL_WIDTH))
    inp['ret_norm'] = gain((DEPTH, RET_WIDTH))
    inp['w_out'] = nrm((DEPTH, D_MIX, D_MODEL), DEEPNORM_BETA * D_MIX ** -0.5)
    inp['ln1_g'] = gain((DEPTH, D_MODEL))
    inp['ln1_b'] = nrm((DEPTH, D_MODEL), 0.02)
    inp['w_up'] = nrm((DEPTH, D_MODEL, 2 * D_FF), D_MODEL ** -0.5)
    inp['conv_w'] = nrm((DEPTH, CONV_W, 2 * D_FF), CONV_W ** -0.5)
    inp['conv_b'] = nrm((DEPTH, 2 * D_FF), 0.02)
    inp['w_down'] = nrm((DEPTH, D_FF, D_MODEL), DEEPNORM_BETA * D_FF ** -0.5)
    inp['w_pe'] = nrm((DEPTH, PLE_DIM, D_MODEL), PLE_DIM ** -0.5)
    inp['w_pg'] = nrm((DEPTH, D_MODEL, D_MODEL), D_MODEL ** -0.5)
    inp['ln2_g'] = gain((DEPTH, D_MODEL))
    inp['ln2_b'] = nrm((DEPTH, D_MODEL), 0.02)
    return inp


def reference(x_prompt, x_sample, p_prompt, p_sample, cache_ckv, cache_kpe, state_ret, state_pool, state_conv,
              page_table, w_in, q_norm, kv_norm, w_uq, w_uk, w_uv, w_pool, pool_scale, ret_norm, w_out,
              ln1_g, ln1_b, w_up, conv_w, conv_b, w_down, w_pe, w_pg, ln2_g, ln2_b):
    B, S, _ = x_prompt.shape
    Bd, T, _ = x_sample.shape
    past_len = page_table.shape[1] * cache_ckv.shape[2]
    pos_p = jnp.arange(S, dtype=jnp.int32)
    pos_s = past_len + jnp.arange(T, dtype=jnp.int32)
    dt = x_prompt.dtype
    attend_p = functools.partial(_mla_prompt, pos=pos_p)
    hp, hs = x_prompt, x_sample
    st_p_all = [[] for _ in range(5)]
    st_s_all = [[] for _ in range(5)]
    for i in range(DEPTH):
        lw = dict(w_in=w_in[i], q_norm=q_norm[i], kv_norm=kv_norm[i], w_uq=w_uq[i], w_uk=w_uk[i], w_uv=w_uv[i],
                  w_pool=w_pool[i], pool_scale=pool_scale[i], ret_norm=ret_norm[i], w_out=w_out[i],
                  ln1_g=ln1_g[i], ln1_b=ln1_b[i], w_up=w_up[i], conv_w=conv_w[i], conv_b=conv_b[i],
                  w_down=w_down[i], w_pe=w_pe[i], w_pg=w_pg[i], ln2_g=ln2_g[i], ln2_b=ln2_b[i])
        hp, st_p = _layer(hp, p_prompt[i], pos_p, lw, attend_p,
                          jnp.zeros((B, RET_HEADS, RET_DIM, RET_DIM), dt), RET_CHUNK,
                          jnp.zeros((B, POOL_BUF, POOL_WIDTH), dt), 0,
                          jnp.zeros((B, CONV_W - 1, 2 * D_FF), dt))
        attend_s = functools.partial(_mla_sample, pos=pos_s, cache_c=cache_ckv[i], cache_p=cache_kpe[i],
                                     page_table=page_table)
        hs, st_s = _layer(hs, p_sample[i], pos_s, lw, attend_s, state_ret[i], T,
                          state_pool[i], min(past_len, POOL_BUF), state_conv[i])
        for lst, v in zip(st_p_all, st_p):
            lst.append(v)
        for lst, v in zip(st_s_all, st_s):
            lst.append(v)
    ckv_p, kpe_p, ret_p, pool_p, conv_p = [jnp.stack(v, 0) for v in st_p_all]
    ckv_s, kpe_s, ret_s, pool_s, conv_s = [jnp.stack(v, 0) for v in st_s_all]
    return (hp, hs, ckv_p, kpe_p, ckv_s, kpe_s, ret_p, ret_s, pool_p, pool_s, conv_p, conv_s)
```

```python
import functools

import jax
import jax.numpy as jnp
from jax import lax
from jax.experimental import pallas as pl
from jax.experimental.pallas import tpu as pltpu

F32, BF16 = jnp.float32, jnp.bfloat16

MLA_HEADS = 16
MLA_NOPE = 128
MLA_ROPE = 64
MLA_V = 128
MLA_QK = MLA_NOPE + MLA_ROPE
QHEAD = 2 * MLA_NOPE
POOL_WINDOWS = (2, 4, 8, 16)
POOL_BUF = max(POOL_WINDOWS) - 1
RET_HEADS = 8
RET_DIM = 128
RET_CHUNK = 128
CONV_W = 3
ROPE_THETA = 10000.0
NEG_INF = -1e30
LANES = 128
SUBLANES = 8
VMEM_LIMIT = 56 * 1024 * 1024


def _cp(*sem):
    return pltpu.CompilerParams(dimension_semantics=sem, vmem_limit_bytes=VMEM_LIMIT)


def _tile(n, pref):
    if n <= pref:
        return n
    t = pref - pref % SUBLANES
    while n % t:
        t -= SUBLANES
    return t


def _nt_dot(a, b):
    return lax.dot_general(a, b, (((1,), (1,)), ((), ())), preferred_element_type=F32)


def _tn_dot(a, b):
    return lax.dot_general(a, b, (((0,), (0,)), ((), ())), preferred_element_type=F32)


def _rope_mix(blk, tc, ts):
    return blk * tc + pltpu.roll(blk, LANES // 2, 1) * ts


def _mm(lhs, ws, *, out_dtype, tm, tn, name):
    m = lhs[0].shape[0]
    n = ws[0][0].shape[1]
    nl = len(lhs)
    tm, tn = _tile(m, tm), _tile(n, tn)

    def body(*refs):
        acc = None
        for l_ref, w_ref in zip(refs[:nl], refs[nl:2 * nl]):
            d = jnp.dot(l_ref[...], w_ref[...], preferred_element_type=F32)
            acc = d if acc is None else acc + d
        refs[-1][...] = acc.astype(out_dtype)

    in_specs = [pl.BlockSpec((tm, a.shape[1]), lambda i, j: (i, 0)) for a in lhs]
    in_specs += [pl.BlockSpec((rows, tn), functools.partial(lambda i, j, rb: (rb, j), rb=rb)) for _, rows, rb in ws]
    return pl.pallas_call(
        body, grid=(m // tm, n // tn), in_specs=in_specs,
        out_specs=pl.BlockSpec((tm, tn), lambda i, j: (i, j)),
        out_shape=jax.ShapeDtypeStruct((m, n), out_dtype),
        compiler_params=_cp("parallel", "arbitrary"), name=name,
    )(*lhs, *[w for w, _, _ in ws])


def _inproj_mla(xb, w, q_norm, kv_norm, tc, ts, *, tm, name):
    m, k = xb.shape
    n = w.shape[1]
    ql, kl = q_norm.shape[-1], kv_norm.shape[-1]
    tm = _tile(m, tm)
    nt = tc.shape[0] // tm

    def body(x_ref, w_ref, qn_ref, kvn_ref, tc_ref, ts_ref, cq_ref, ckv_ref, kpe_ref):
        acc = jnp.dot(x_ref[...], w_ref[...], preferred_element_type=F32)
        cq, ckv, kb = acc[:, :ql], acc[:, ql:ql + kl], acc[:, ql + kl:]
        cq = cq * lax.rsqrt(jnp.mean(cq * cq, -1, keepdims=True) + 1e-6)
        cq_ref[...] = (cq * qn_ref[...]).astype(BF16)
        ckv = ckv * lax.rsqrt(jnp.mean(ckv * ckv, -1, keepdims=True) + 1e-6)
        ckv_ref[...] = ckv * kvn_ref[...]
        kpe_ref[...] = _rope_mix(kb, tc_ref[...], ts_ref[...])

    return pl.pallas_call(
        body, grid=(m // tm,),
        in_specs=[pl.BlockSpec((tm, k), lambda i: (i, 0)), pl.BlockSpec((k, n), lambda i: (0, 0)),
                  pl.BlockSpec((1, ql), lambda i: (0, 0)), pl.BlockSpec((1, kl), lambda i: (0, 0)),
                  pl.BlockSpec((tm, LANES), lambda i: (i % nt, 0)), pl.BlockSpec((tm, LANES), lambda i: (i % nt, 0))],
        out_specs=[pl.BlockSpec((tm, ql), lambda i: (i, 0)), pl.BlockSpec((tm, kl), lambda i: (i, 0)),
                   pl.BlockSpec((tm, LANES), lambda i: (i, 0))],
        out_shape=[jax.ShapeDtypeStruct((m, ql), BF16), jax.ShapeDtypeStruct((m, kl), F32),
                   jax.ShapeDtypeStruct((m, LANES), F32)],
        compiler_params=_cp("parallel"), name=name,
    )(xb, w, q_norm, kv_norm, tc, ts)


def _qproj(cq, w, tc, ts, *, tm, tn, name):
    m, k = cq.shape
    n = w.shape[1]
    tm, tn = _tile(m, tm), _tile(n, tn)
    nt = tc.shape[0] // tm

    def body(c_ref, w_ref, tc_ref, ts_ref, q_ref):
        acc = jnp.dot(c_ref[...], w_ref[...], preferred_element_type=F32)
        tcv, tsv = tc_ref[...], ts_ref[...]
        for hh in range(tn // QHEAD):
            a = hh * QHEAD
            q_ref[:, a:a + LANES] = acc[:, a:a + LANES].astype(BF16)
            q_ref[:, a + LANES:a + QHEAD] = _rope_mix(acc[:, a + LANES:a + QHEAD], tcv, tsv).astype(BF16)

    return pl.pallas_call(
        body, grid=(m // tm, n // tn),
        in_specs=[pl.BlockSpec((tm, k), lambda i, j: (i, 0)), pl.BlockSpec((k, tn), lambda i, j: (0, j)),
                  pl.BlockSpec((tm, LANES), lambda i, j: (i % nt, 0)), pl.BlockSpec((tm, LANES), lambda i, j: (i % nt, 0))],
        out_specs=pl.BlockSpec((tm, tn), lambda i, j: (i, j)),
        out_shape=jax.ShapeDtypeStruct((m, n), BF16),
        compiler_params=_cp("parallel", "arbitrary"), name=name,
    )(cq, w, tc, ts)


def _kv_expand(ckv, kpe, w_ukv, *, tm, name):
    m, kl = ckv.shape
    hn = MLA_HEADS * MLA_NOPE
    tm = _tile(m, tm)

    def body(c_ref, p_ref, w_ref, k_ref, v_ref):
        acc = jnp.dot(c_ref[...].astype(BF16), w_ref[...], preferred_element_type=F32)
        pb = p_ref[...].astype(BF16)
        for h in range(MLA_HEADS):
            k_ref[:, h * QHEAD:h * QHEAD + LANES] = acc[:, h * MLA_NOPE:(h + 1) * MLA_NOPE].astype(BF16)
            k_ref[:, h * QHEAD + LANES:(h + 1) * QHEAD] = pb
        v_ref[...] = acc[:, hn:].astype(BF16)

    return pl.pallas_call(
        body, grid=(m // tm,),
        in_specs=[pl.BlockSpec((tm, kl), lambda i: (i, 0)), pl.BlockSpec((tm, LANES), lambda i: (i, 0)),
                  pl.BlockSpec(w_ukv.shape, lambda i: (0, 0))],
        out_specs=[pl.BlockSpec((tm, MLA_HEADS * QHEAD), lambda i: (i, 0)), pl.BlockSpec((tm, MLA_HEADS * MLA_V), lambda i: (i, 0))],
        out_shape=[jax.ShapeDtypeStruct((m, MLA_HEADS * QHEAD), BF16), jax.ShapeDtypeStruct((m, MLA_HEADS * MLA_V), BF16)],
        compiler_params=_cp("parallel"), name=name,
    )(ckv, kpe, w_ukv)


def _head_mm(lhs, w, *, lhs_head_major, out_head_major, lhs_cols, lhs_colblock, name):
    hcount, kdim, n = w.shape
    m = lhs.shape[1] if lhs_head_major else lhs.shape[0]

    def body(l_ref, w_ref, o_ref):
        o_ref[...] = jnp.dot(l_ref[...].astype(BF16), w_ref[...], preferred_element_type=F32).astype(o_ref.dtype)

    if lhs_head_major:
        l_spec = pl.BlockSpec((None, m, kdim), lambda h: (h, 0, 0))
    else:
        l_spec = pl.BlockSpec((m, lhs_cols), lambda h: (0, lhs_colblock(h)))
    if out_head_major:
        o_spec, o_shape = pl.BlockSpec((None, m, n), lambda h: (h, 0, 0)), (hcount, m, n)
    else:
        o_spec, o_shape = pl.BlockSpec((m, n), lambda h: (0, h)), (m, hcount * n)
    return pl.pallas_call(
        body, grid=(hcount,), in_specs=[l_spec, pl.BlockSpec((None, kdim, n), lambda h: (h, 0, 0))],
        out_specs=o_spec, out_shape=jax.ShapeDtypeStruct(o_shape, BF16),
        compiler_params=_cp("parallel"), name=name,
    )(lhs, w)


def _prompt_attn(q, k, v, *, tq, name):
    b, s, _ = q.shape
    tq = _tile(s, tq)
    nq = s // tq
    scale = MLA_QK ** -0.5

    def body(q_ref, k_ref, v_ref, o_ref, m_sc, l_sc, acc_sc):
        qi, ki = pl.program_id(2), pl.program_id(3)

        @pl.when(ki == 0)
        def _():
            m_sc[...] = jnp.full(m_sc.shape, NEG_INF, F32)
            l_sc[...] = jnp.zeros(l_sc.shape, F32)
            acc_sc[...] = jnp.zeros(acc_sc.shape, F32)

        @pl.when(ki <= qi)
        def _():
            sc = _nt_dot(q_ref[0], k_ref[0]) * scale
            row = qi * tq + lax.broadcasted_iota(jnp.int32, (tq, tq), 0)
            col = ki * tq + lax.broadcasted_iota(jnp.int32, (tq, tq), 1)
            sc = jnp.where(col <= row, sc, NEG_INF)
            m_new = jnp.maximum(m_sc[...], jnp.max(sc, -1, keepdims=True))
            alpha = jnp.exp(m_sc[...] - m_new)
            p = jnp.exp(sc - m_new)
            l_sc[...] = alpha * l_sc[...] + jnp.sum(p, -1, keepdims=True)
            acc_sc[...] = alpha * acc_sc[...] + jnp.dot(p.astype(BF16), v_ref[0], preferred_element_type=F32)
            m_sc[...] = m_new

        @pl.when(ki == qi)
        def _():
            o_ref[0] = (acc_sc[...] / l_sc[...]).astype(BF16)

    return pl.pallas_call(
        body, grid=(b, MLA_HEADS, nq, nq),
        in_specs=[pl.BlockSpec((1, tq, QHEAD), lambda bi, h, qi, ki: (bi, qi, h)),
                  pl.BlockSpec((1, tq, QHEAD), lambda bi, h, qi, ki: (bi, jnp.minimum(ki, qi), h)),
                  pl.BlockSpec((1, tq, MLA_V), lambda bi, h, qi, ki: (bi, jnp.minimum(ki, qi), h))],
        out_specs=pl.BlockSpec((1, tq, MLA_V), lambda bi, h, qi, ki: (bi, qi, h)),
        out_shape=jax.ShapeDtypeStruct((b, s, MLA_HEADS * MLA_V), BF16),
        scratch_shapes=[pltpu.VMEM((tq, 1), F32), pltpu.VMEM((tq, 1), F32), pltpu.VMEM((tq, MLA_V), F32)],
        compiler_params=_cp("parallel", "parallel", "parallel", "arbitrary"), name=name,
    )(q, k, v)


def _decode_attn(qlat, qpe, cnew, pnew, vis, cache_c, cache_p, page_table, *, layer, pps, name):
    bd, r, c = qlat.shape
    rp = qpe.shape[-1]
    page = cache_c.shape[2]
    n_pages = page_table.shape[1]
    pps = _tile(n_pages, pps) if n_pages % pps else pps
    nj = n_pages // pps
    scale = MLA_QK ** -0.5
    pad_new = cnew.shape[1]

    def body(pt_ref, ql_ref, qp_ref, cn_ref, pn_ref, vis_ref, *rest):
        ck, kp = rest[:pps], rest[pps:2 * pps]
        o_ref, kbuf, pbuf, m_sc, l_sc, acc_sc = rest[2 * pps:]
        j = pl.program_id(1)

        @pl.when(j == 0)
        def _():
            m_sc[...] = jnp.full(m_sc.shape, NEG_INF, F32)
            l_sc[...] = jnp.zeros(l_sc.shape, F32)
            acc_sc[...] = jnp.zeros(acc_sc.shape, F32)

        for kk in range(pps):
            kbuf[kk * page:(kk + 1) * page, :] = ck[kk][...].astype(BF16)
            pbuf[kk * page:(kk + 1) * page, :] = kp[kk][...].astype(BF16)
        ql, qp = ql_ref[0], qp_ref[0]

        def update(sc, vals):
            m_new = jnp.maximum(m_sc[...], jnp.max(sc, -1, keepdims=True))
            alpha = jnp.exp(m_sc[...] - m_new)
            p = jnp.exp(sc - m_new)
            l_sc[...] = alpha * l_sc[...] + jnp.sum(p, -1, keepdims=True)
            acc_sc[...] = alpha * acc_sc[...] + jnp.dot(p.astype(BF16), vals, preferred_element_type=F32)
            m_sc[...] = m_new

        kb = kbuf[...]
        update((_nt_dot(ql, kb) + _nt_dot(qp, pbuf[...])) * scale, kb)

        @pl.when(j == nj - 1)
        def _():
            cn, pn = cn_ref[0].astype(BF16), pn_ref[0].astype(BF16)
            sc = (_nt_dot(ql, cn) + _nt_dot(qp, pn)) * scale
            sc = jnp.where(vis_ref[...] > 0, sc, NEG_INF)
            update(sc, cn)
            o_ref[0] = (acc_sc[...] / l_sc[...]).astype(o_ref.dtype)

    def page_map(kk):
        return lambda s, j, pt: (layer, pt[s, j * pps + kk], 0, 0)

    in_specs = [pl.BlockSpec((1, r, c), lambda s, j, pt: (s, 0, 0)), pl.BlockSpec((1, r, rp), lambda s, j, pt: (s, 0, 0)),
                pl.BlockSpec((1, pad_new, c), lambda s, j, pt: (s, 0, 0)), pl.BlockSpec((1, pad_new, rp), lambda s, j, pt: (s, 0, 0)),
                pl.BlockSpec((r, pad_new), lambda s, j, pt: (0, 0))]
    in_specs += [pl.BlockSpec((None, None, page, c), page_map(kk)) for kk in range(pps)]
    in_specs += [pl.BlockSpec((None, None, page, rp), page_map(kk)) for kk in range(pps)]
    grid_spec = pltpu.PrefetchScalarGridSpec(
        num_scalar_prefetch=1, grid=(bd, nj), in_specs=in_specs,
        out_specs=pl.BlockSpec((1, r, c), lambda s, j, pt: (s, 0, 0)),
        scratch_shapes=[pltpu.VMEM((pps * page, c), BF16), pltpu.VMEM((pps * page, rp), BF16),
                        pltpu.VMEM((r, 1), F32), pltpu.VMEM((r, 1), F32), pltpu.VMEM((r, c), F32)])
    return pl.pallas_call(
        body, grid_spec=grid_spec, out_shape=jax.ShapeDtypeStruct((bd, r, c), BF16),
        compiler_params=_cp("parallel", "arbitrary"), name=name,
    )(page_table, qlat, qpe, cnew, pnew, vis, *([cache_c] * pps), *([cache_p] * pps))


def _retention(h_rest, tc, ts, d_intra, d_q, d_k, d_c, s0, ret_norm, *, layer, chunk, col0, name):
    b, l, _ = h_rest.shape
    w = RET_HEADS * RET_DIM
    nc = l // chunk
    k_scale = RET_DIM ** -0.5

    def body(q_ref, k_ref, v_ref, g_ref, tc_ref, ts_ref, di_ref, dq_ref, dk_ref, dc_ref, s0_ref, gn_ref, o_ref, sn_ref, s_sc):
        ci = pl.program_id(1)

        @pl.when(ci == 0)
        def _():
            s_sc[...] = s0_ref[0]

        tcv, tsv = tc_ref[...], ts_ref[...]
        for h in range(RET_HEADS):
            hs = slice(h * RET_DIM, (h + 1) * RET_DIM)
            qf = _rope_mix(q_ref[0, :, hs], tcv, tsv)
            kf = _rope_mix(k_ref[0, :, hs], tcv, tsv) * k_scale
            qb, vb = qf.astype(BF16), v_ref[0, :, hs].astype(BF16)
            st = s_sc[h]
            sc = _nt_dot(qb, kf.astype(BF16)) * di_ref[h]
            o = jnp.dot(sc.astype(BF16), vb, preferred_element_type=F32)
            o = o + jnp.dot(qb, st.astype(BF16), preferred_element_type=F32) * dq_ref[h]
            s_sc[h] = st * dc_ref[h, 0:1, :] + _tn_dot((kf * dk_ref[h]).astype(BF16), vb)
            mu = jnp.mean(o, -1, keepdims=True)
            var = jnp.mean(jnp.square(o - mu), -1, keepdims=True)
            y = (o - mu) * lax.rsqrt(var + 1e-5) * gn_ref[:, hs]
            gate = g_ref[0, :, hs]
            o_ref[0, :, hs] = (gate * jax.nn.sigmoid(gate) * y).astype(BF16)

        @pl.when(ci == nc - 1)
        def _():
            sn_ref[0] = s_sc[...]

    def colspec(cb):
        return pl.BlockSpec((1, chunk, w), lambda bi, ci: (bi, ci, cb))

    full3 = lambda a: pl.BlockSpec(a.shape, lambda bi, ci: (0, 0, 0))
    return pl.pallas_call(
        body, grid=(b, nc),
        in_specs=[colspec(col0), colspec(col0 + 1), colspec(col0 + 2), colspec(col0 + 3),
                  pl.BlockSpec((chunk, LANES), lambda bi, ci: (ci, 0)), pl.BlockSpec((chunk, LANES), lambda bi, ci: (ci, 0)),
                  full3(d_intra), full3(d_q), full3(d_k), full3(d_c),
                  pl.BlockSpec((None, 1, RET_HEADS, RET_DIM, RET_DIM), lambda bi, ci: (layer, bi, 0, 0, 0)),
                  pl.BlockSpec((1, w), lambda bi, ci: (0, 0))],
        out_specs=[pl.BlockSpec((1, chunk, w), lambda bi, ci: (bi, ci, 0)),
                   pl.BlockSpec((1, RET_HEADS, RET_DIM, RET_DIM), lambda bi, ci: (bi, 0, 0, 0))],
        out_shape=[jax.ShapeDtypeStruct((b, l, w), BF16), jax.ShapeDtypeStruct((b, RET_HEADS, RET_DIM, RET_DIM), F32)],
        scratch_shapes=[pltpu.VMEM((RET_HEADS, RET_DIM, RET_DIM), F32)],
        compiler_params=_cp("parallel", "arbitrary"), name=name,
    )(h_rest, h_rest, h_rest, h_rest, tc, ts, d_intra, d_q, d_k, d_c, s0, ret_norm)


def _ret_tables(chunk, valid):
    log_g = jnp.log1p(-jnp.exp2(-5.0 - jnp.arange(RET_HEADS, dtype=F32)))
    idx = jnp.arange(chunk, dtype=F32)
    ok = idx < valid
    diff = idx[:, None] - idx[None, :]
    d_intra = jnp.where((diff >= 0) & ok[:, None] & ok[None, :],
                        jnp.exp(log_g[:, None, None] * jnp.maximum(diff, 0.0)), 0.0)
    d_q = jnp.where(ok[None, :], jnp.exp(log_g[:, None] * (idx[None, :] + 1.0)), 0.0)
    d_k = jnp.where(ok[None, :], jnp.exp(log_g[:, None] * (valid - 1.0 - idx[None, :])), 0.0)
    d_c = jnp.exp(log_g * valid)
    bc = lambda a: jnp.broadcast_to(a[..., None], a.shape + (RET_DIM,))
    return d_intra, bc(d_q), bc(d_k), jnp.broadcast_to(d_c[:, None, None], (RET_HEADS, SUBLANES, RET_DIM))


def _pool_prompt(ext, w_pool, pool_scale, *, past_valid, name):
    b, le, w = ext.shape
    pad = POOL_BUF + 1
    l = le - pad
    gd = w // len(POOL_WINDOWS)
    tc = _tile(l, 256)

    def body(x_ref, w_ref, sc_ref, o_ref):
        for c in range(l // tc):
            for g, win in enumerate(POOL_WINDOWS):
                gs = slice(g * gd, (g + 1) * gd)
                x = x_ref[0, c * tc:c * tc + tc + pad, gs]
                acc, span = x, 1
                while span < win:
                    acc = acc + pltpu.roll(acc, span, 0)
                    span *= 2
                t = (c * tc + lax.broadcasted_iota(jnp.int32, (tc, 1), 0)).astype(F32)
                cnt = jnp.minimum(t + 1.0 + past_valid, float(win))
                d = acc[pad:] / cnt - x[pad:]
                y = jnp.dot(d.astype(BF16), w_ref[g], preferred_element_type=F32) * sc_ref[:, gs]
                o_ref[0, c * tc:(c + 1) * tc, gs] = y.astype(BF16)

    return pl.pallas_call(
        body, grid=(b,),
        in_specs=[pl.BlockSpec((1, le, w), lambda bi: (bi, 0, 0)), pl.BlockSpec(w_pool.shape, lambda bi: (0, 0, 0)),
                  pl.BlockSpec((1, w), lambda bi: (0, 0))],
        out_specs=pl.BlockSpec((1, l, w), lambda bi: (bi, 0, 0)),
        out_shape=jax.ShapeDtypeStruct((b, l, w), BF16),
        compiler_params=_cp("parallel"), name=name,
    )(ext, w_pool, pool_scale)


def _pool_sample(ext, w_pool, pool_scale, *, n_new, past_valid, name):
    le, bd, w = ext.shape
    gd = w // len(POOL_WINDOWS)
    bs = _tile(bd, 64)

    def body(x_ref, w_ref, sc_ref, o_ref):
        for t in range(n_new):
            for g, win in enumerate(POOL_WINDOWS):
                gs = slice(g * gd, (g + 1) * gd)
                x = x_ref[POOL_BUF + t, :, gs]
                acc = x
                for k in range(1, win):
                    acc = acc + x_ref[POOL_BUF + t - k, :, gs]
                cnt = min(t + 1.0 + past_valid, float(win))
                d = acc / cnt - x
                y = jnp.dot(d.astype(BF16), w_ref[g], preferred_element_type=F32) * sc_ref[:, gs]
                o_ref[t, :, gs] = y.astype(BF16)

    return pl.pallas_call(
        body, grid=(bd // bs,),
        in_specs=[pl.BlockSpec((le, bs, w), lambda i: (0, i, 0)), pl.BlockSpec(w_pool.shape, lambda i: (0, 0, 0)),
                  pl.BlockSpec((1, w), lambda i: (0, 0))],
        out_specs=pl.BlockSpec((n_new, bs, w), lambda i: (0, i, 0)),
        out_shape=jax.ShapeDtypeStruct((n_new, bd, w), BF16),
        compiler_params=_cp("parallel"), name=name,
    )(ext, w_pool, pool_scale)


def _ln_body_core(z, g, b):
    mu = jnp.mean(z, -1, keepdims=True)
    var = jnp.mean(jnp.square(z - mu), -1, keepdims=True)
    return (z - mu) * lax.rsqrt(var + 1e-5) * g + b


def _ln1(x, mix, g, b, *, alpha, tm, name):
    m, d = x.shape
    tm = _tile(m, tm)

    def body(x_ref, m_ref, g_ref, b_ref, o_ref, ob_ref):
        y = _ln_body_core(alpha * x_ref[...] + m_ref[...], g_ref[...], b_ref[...])
        o_ref[...] = y
        ob_ref[...] = y.astype(BF16)

    row = pl.BlockSpec((tm, d), lambda i: (i, 0))
    vec = pl.BlockSpec((1, d), lambda i: (0, 0))
    return pl.pallas_call(
        body, grid=(m // tm,), in_specs=[row, row, vec, vec], out_specs=[row, row],
        out_shape=[jax.ShapeDtypeStruct((m, d), F32), jax.ShapeDtypeStruct((m, d), BF16)],
        compiler_params=_cp("parallel"), name=name,
    )(x, mix, g, b)


def _ln2(x, f, pe, pg, g, b, *, alpha, tm, name):
    m, d = x.shape
    tm = _tile(m, tm)

    def body(x_ref, f_ref, pe_ref, pg_ref, g_ref, b_ref, o_ref, ob_ref):
        z = alpha * x_ref[...] + f_ref[...] + pe_ref[...] * jax.nn.sigmoid(pg_ref[...])
        y = _ln_body_core(z, g_ref[...], b_ref[...])
        o_ref[...] = y
        ob_ref[...] = y.astype(BF16)

    row = pl.BlockSpec((tm, d), lambda i: (i, 0))
    vec = pl.BlockSpec((1, d), lambda i: (0, 0))
    return pl.pallas_call(
        body, grid=(m // tm,), in_specs=[row, row, row, row, vec, vec], out_specs=[row, row],
        out_shape=[jax.ShapeDtypeStruct((m, d), F32), jax.ShapeDtypeStruct((m, d), BF16)],
        compiler_params=_cp("parallel"), name=name,
    )(x, f, pe, pg, g, b)


def _conv_gate_prompt(u, conv_w, conv_b, *, seq, tm, tn, name):
    m, f2 = u.shape
    f = f2 // 2
    tm, tn = _tile(seq, tm), _tile(f, tn)
    nb = f // tn
    tps = seq // tm
    hb = tm // SUBLANES

    def body(ua_ref, ub_ref, ha_ref, hb_ref, wa_ref, wb_ref, ba_ref, bb_ref, o_ref, ea_sc, eb_sc):
        first = pl.program_id(0) % tps == 0

        def conv(u_ref, h_ref, w_ref, b_ref, e_sc):
            e_sc[0:SUBLANES, :] = jnp.where(first, 0.0, h_ref[...])
            e_sc[SUBLANES:, :] = u_ref[...]
            y = b_ref[...] + e_sc[pl.ds(SUBLANES - 2, tm), :] * w_ref[0:1, :]
            y = y + e_sc[pl.ds(SUBLANES - 1, tm), :] * w_ref[1:2, :]
            return y + u_ref[...] * w_ref[2:3, :]

        ya = conv(ua_ref, ha_ref, wa_ref, ba_ref, ea_sc)
        yb = conv(ub_ref, hb_ref, wb_ref, bb_ref, eb_sc)
        o_ref[...] = (ya * jax.nn.sigmoid(ya) * yb).astype(BF16)

    halo = lambda i: jnp.maximum(i * hb - 1, 0)
    return pl.pallas_call(
        body, grid=(m // tm, nb),
        in_specs=[pl.BlockSpec((tm, tn), lambda i, j: (i, j)), pl.BlockSpec((tm, tn), lambda i, j: (i, j + nb)),
                  pl.BlockSpec((SUBLANES, tn), lambda i, j: (halo(i), j)), pl.BlockSpec((SUBLANES, tn), lambda i, j: (halo(i), j + nb)),
                  pl.BlockSpec((CONV_W, tn), lambda i, j: (0, j)), pl.BlockSpec((CONV_W, tn), lambda i, j: (0, j + nb)),
                  pl.BlockSpec((1, tn), lambda i, j: (0, j)), pl.BlockSpec((1, tn), lambda i, j: (0, j + nb))],
        out_specs=pl.BlockSpec((tm, tn), lambda i, j: (i, j)),
        out_shape=jax.ShapeDtypeStruct((m, f), BF16),
        scratch_shapes=[pltpu.VMEM((tm + SUBLANES, tn), F32), pltpu.VMEM((tm + SUBLANES, tn), F32)],
        compiler_params=_cp("parallel", "parallel"), name=name,
    )(u, u, u, u, conv_w, conv_w, conv_b, conv_b)


def _conv_gate_sample(u, buf, conv_w, conv_b, *, bd, tn, name):
    m, f2 = u.shape
    f = f2 // 2
    tn = _tile(f, tn)
    nb = f // tn

    def body(ua_ref, ub_ref, ha_ref, hb_ref, wa_ref, wb_ref, ba_ref, bb_ref, o_ref):
        def conv(u_ref, h_ref, w_ref, b_ref):
            ext = jnp.concatenate([h_ref[...], u_ref[...]], axis=0)
            y = b_ref[...] + ext[0:m] * w_ref[0:1, :]
            y = y + ext[bd:bd + m] * w_ref[1:2, :]
            return y + ext[2 * bd:] * w_ref[2:3, :]

        ya = conv(ua_ref, ha_ref, wa_ref, ba_ref)
        yb = conv(ub_ref, hb_ref, wb_ref, bb_ref)
        o_ref[...] = (ya * jax.nn.sigmoid(ya) * yb).astype(BF16)

    return pl.pallas_call(
        body, grid=(nb,),
        in_specs=[pl.BlockSpec((m, tn), lambda j: (0, j)), pl.BlockSpec((m, tn), lambda j: (0, j + nb)),
                  pl.BlockSpec((2 * bd, tn), lambda j: (0, j)), pl.BlockSpec((2 * bd, tn), lambda j: (0, j + nb)),
                  pl.BlockSpec((CONV_W, tn), lambda j: (0, j)), pl.BlockSpec((CONV_W, tn), lambda j: (0, j + nb)),
                  pl.BlockSpec((1, tn), lambda j: (0, j)), pl.BlockSpec((1, tn), lambda j: (0, j + nb))],
        out_specs=pl.BlockSpec((m, tn), lambda j: (0, j)),
        out_shape=jax.ShapeDtypeStruct((m, f), BF16),
        compiler_params=_cp("parallel"), name=name,
    )(u, u, buf, buf, conv_w, conv_w, conv_b, conv_b)


def _rope_tables(pos, d):
    inv = ROPE_THETA ** (-jnp.arange(0, d, 2, dtype=F32) / d)
    ang = pos.astype(F32)[:, None] * inv[None, :]
    return jnp.cos(ang), jnp.sin(ang)


def _tables(pos):
    c64, s64 = _rope_tables(pos, MLA_ROPE)
    z = jnp.zeros((pos.shape[0], LANES // 2), F32)
    c128, s128 = _rope_tables(pos, RET_DIM)
    return (jnp.concatenate([c64, c64, z], 1), jnp.concatenate([s64, s64, z], 1),
            jnp.concatenate([c128, c128], 1), jnp.concatenate([-s128, s128], 1))


def _rot_cols(w):
    half = w.shape[-1] // 2
    return jnp.concatenate([-w[..., half:], w[..., :half]], -1)


def kernel(x_prompt, x_sample, p_prompt, p_sample, cache_ckv, cache_kpe, state_ret, state_pool, state_conv, page_table, w_in, q_norm, kv_norm, w_uq, w_uk, w_uv, w_pool, pool_scale, ret_norm, w_out, ln1_g, ln1_b, w_up, conv_w, conv_b, w_down, w_pe, w_pg, ln2_g, ln2_b):
    depth = w_in.shape[0]
    b, s, d_model = x_prompt.shape
    bd, t_new, _ = x_sample.shape
    page = cache_ckv.shape[2]
    past = page_table.shape[1] * page
    ql, kl = q_norm.shape[-1], kv_norm.shape[-1]
    pool_w = pool_scale.shape[-1]
    ret_w = ret_norm.shape[-1]
    alpha = (2 * depth) ** 0.25
    mp, ms = b * s, bd * t_new
    pad_new = SUBLANES

    o2, o3 = ql + kl, ql + kl + MLA_ROPE

    def layer_weights(i):
        w_kpe = w_in[i, :, o2:o3]
        wq = w_uq[i].reshape(ql, MLA_HEADS, MLA_QK)
        wq_rope = wq[..., MLA_NOPE:]
        return dict(
            mla=jnp.concatenate([w_in[i, :, :o2], w_kpe, _rot_cols(w_kpe)], -1).astype(BF16),
            rest=w_in[i, :, o3:].astype(BF16),
            q=jnp.concatenate([wq[..., :MLA_NOPE], wq_rope, _rot_cols(wq_rope)], -1).reshape(ql, MLA_HEADS * QHEAD).astype(BF16),
            ukv=jnp.concatenate([w_uk[i].reshape(kl, -1), w_uv[i].reshape(kl, -1)], -1).astype(BF16),
            uk_t=jnp.transpose(w_uk[i], (1, 2, 0)).astype(BF16),
            uv_h=jnp.transpose(w_uv[i], (1, 0, 2)).astype(BF16),
            pool=w_pool[i].astype(BF16), out=w_out[i].astype(BF16), up=w_up[i].astype(BF16),
            down=w_down[i].astype(BF16), pe=w_pe[i].astype(BF16), pg=w_pg[i].astype(BF16))

    pos_p = jnp.arange(s, dtype=jnp.int32)
    pos_s = jnp.repeat(past + jnp.arange(t_new, dtype=jnp.int32), bd)
    tcp, tsp, rcp, rsp = _tables(pos_p)
    tcs, tss, _, _ = _tables(pos_s)
    pos_s8 = past + jnp.arange(pad_new, dtype=jnp.int32)
    _, _, rcs, rss = _tables(pos_s8)
    ret_tab_p = _ret_tables(RET_CHUNK, RET_CHUNK)
    ret_tab_s = _ret_tables(pad_new, t_new)
    zero_state = jnp.zeros((1, b, RET_HEADS, RET_DIM, RET_DIM), F32)
    r_idx = jnp.arange(MLA_HEADS * t_new, dtype=jnp.int32)[:, None] % t_new
    k_idx = jnp.arange(pad_new, dtype=jnp.int32)[None, :]
    vis = ((k_idx <= r_idx) & (k_idx < t_new)).astype(jnp.int32)

    xp = x_prompt.reshape(mp, d_model)
    xs = jnp.transpose(x_sample, (1, 0, 2)).reshape(ms, d_model)
    xpb, xsb = xp.astype(BF16), xs.astype(BF16)
    pp = p_prompt.reshape(depth, mp, -1).astype(BF16)
    ps = jnp.transpose(p_sample, (0, 2, 1, 3)).reshape(depth, ms, -1).astype(BF16)

    outs_p = [[] for _ in range(5)]
    outs_s = [[] for _ in range(5)]
    row = lambda a: a.reshape(1, -1)
    pv_s = float(min(past, POOL_BUF))

    for i in range(depth):
        lw = layer_weights(i)
        wout_parts = [(lw["out"], MLA_HEADS * MLA_V, 0), (lw["out"], pool_w, MLA_HEADS * MLA_V // pool_w),
                      (lw["out"], ret_w, (MLA_HEADS * MLA_V + pool_w) // ret_w)]
        cq, ckv, kpe = _inproj_mla(xpb, lw["mla"], row(q_norm[i]), row(kv_norm[i]), tcp, tsp, tm=256, name=f"p_inmla{i}")
        hr = _mm([xpb], [(lw["rest"], d_model, 0)], out_dtype=F32, tm=1024, tn=512, name=f"p_inrest{i}")
        q = _qproj(cq, lw["q"], tcp, tsp, tm=512, tn=1024, name=f"p_q{i}")
        kx, vx = _kv_expand(ckv, kpe, lw["ukv"], tm=256, name=f"p_kv{i}")
        o_mla = _prompt_attn(q.reshape(b, s, -1), kx.reshape(b, s, -1), vx.reshape(b, s, -1), tq=512, name=f"p_attn{i}")
        hr3 = hr.reshape(b, s, -1)
        ext = jnp.pad(hr3[:, :, :pool_w], ((0, 0), (POOL_BUF + 1, 0), (0, 0)))
        o_pool = _pool_prompt(ext, lw["pool"], row(pool_scale[i]), past_valid=0.0, name=f"p_pool{i}")
        o_ret, ret_new = _retention(hr3, rcp, rsp, *ret_tab_p, zero_state, row(ret_norm[i]), layer=0,
                                    chunk=RET_CHUNK, col0=pool_w // ret_w, name=f"p_ret{i}")
        mix = _mm([o_mla.reshape(mp, -1), o_pool.reshape(mp, -1), o_ret.reshape(mp, -1)], wout_parts,
                  out_dtype=F32, tm=1024, tn=512, name=f"p_out{i}")
        x1, x1b = _ln1(xp, mix, row(ln1_g[i]), row(ln1_b[i]), alpha=alpha, tm=256, name=f"p_ln1{i}")
        u = _mm([x1b], [(lw["up"], d_model, 0)], out_dtype=F32, tm=1024, tn=512, name=f"p_up{i}")
        g = _conv_gate_prompt(u, conv_w[i], row(conv_b[i]), seq=s, tm=512, tn=1024, name=f"p_gate{i}")
        f = _mm([g], [(lw["down"], g.shape[1], 0)], out_dtype=F32, tm=512, tn=512, name=f"p_down{i}")
        pg = _mm([x1b], [(lw["pg"], d_model, 0)], out_dtype=F32, tm=1024, tn=512, name=f"p_pg{i}")
        pe = _mm([pp[i]], [(lw["pe"], pp.shape[-1], 0)], out_dtype=F32, tm=1024, tn=512, name=f"p_pe{i}")
        xp, xpb = _ln2(x1, f, pe, pg, row(ln2_g[i]), row(ln2_b[i]), alpha=alpha, tm=256, name=f"p_ln2{i}")
        outs_p[0].append(ckv.reshape(b, s, kl))
        outs_p[1].append(kpe[:, :MLA_ROPE].reshape(b, s, MLA_ROPE))
        outs_p[2].append(ret_new)
        outs_p[3].append(hr3[:, s - POOL_BUF:, :pool_w])
        outs_p[4].append(u.reshape(b, s, -1)[:, s - (CONV_W - 1):])

        cq, ckv, kpe = _inproj_mla(xsb, lw["mla"], row(q_norm[i]), row(kv_norm[i]), tcs, tss, tm=256, name=f"s_inmla{i}")
        hr = _mm([xsb], [(lw["rest"], d_model, 0)], out_dtype=F32, tm=512, tn=512, name=f"s_inrest{i}")
        q = _qproj(cq, lw["q"], tcs, tss, tm=512, tn=1024, name=f"s_q{i}")
        qlat = _head_mm(q, lw["uk_t"], lhs_head_major=False, out_head_major=True, lhs_cols=MLA_NOPE,
                        lhs_colblock=lambda h: 2 * h, name=f"s_qlat{i}")
        qlat = jnp.transpose(qlat.reshape(MLA_HEADS, t_new, bd, kl), (2, 0, 1, 3)).reshape(bd, MLA_HEADS * t_new, kl)
        qpe = q.reshape(t_new, bd, MLA_HEADS, QHEAD)[..., MLA_NOPE:MLA_QK]
        qpe = jnp.transpose(qpe, (1, 2, 0, 3)).reshape(bd, MLA_HEADS * t_new, MLA_ROPE)
        ckv_s = jnp.transpose(ckv.reshape(t_new, bd, kl), (1, 0, 2))
        kpe_s = jnp.transpose(kpe[:, :MLA_ROPE].reshape(t_new, bd, MLA_ROPE), (1, 0, 2))
        cnew = jnp.pad(ckv_s, ((0, 0), (0, pad_new - t_new), (0, 0)))
        pnew = jnp.pad(kpe_s, ((0, 0), (0, pad_new - t_new), (0, 0)))
        o_lat = _decode_attn(qlat, qpe, cnew, pnew, vis, cache_ckv, cache_kpe, page_table, layer=i, pps=8,
                             name=f"s_attn{i}")
        o_lat = jnp.transpose(o_lat.reshape(bd, MLA_HEADS, t_new, kl), (1, 2, 0, 3)).reshape(MLA_HEADS, ms, kl)
        o_mla = _head_mm(o_lat, lw["uv_h"], lhs_head_major=True, out_head_major=False, lhs_cols=None, lhs_colblock=None,
                         name=f"s_ouv{i}")
        hr3 = hr.reshape(t_new, bd, -1)
        ext = jnp.concatenate([jnp.transpose(state_pool[i], (1, 0, 2)), hr3[:, :, :pool_w]], 0)
        o_pool = _pool_sample(ext, lw["pool"], row(pool_scale[i]), n_new=t_new, past_valid=pv_s, name=f"s_pool{i}")
        hr_b = jnp.pad(jnp.transpose(hr3, (1, 0, 2)), ((0, 0), (0, pad_new - t_new), (0, 0)))
        o_ret, ret_new = _retention(hr_b, rcs, rss, *ret_tab_s, state_ret, row(ret_norm[i]), layer=i,
                                    chunk=pad_new, col0=pool_w // ret_w, name=f"s_ret{i}")
        o_ret = jnp.transpose(o_ret[:, :t_new], (1, 0, 2)).reshape(ms, -1)
        mix = _mm([o_mla, o_pool.reshape(ms, -1), o_ret], wout_parts, out_dtype=F32, tm=512, tn=512, name=f"s_out{i}")
        x1, x1b = _ln1(xs, mix, row(ln1_g[i]), row(ln1_b[i]), alpha=alpha, tm=256, name=f"s_ln1{i}")
        u = _mm([x1b], [(lw["up"], d_model, 0)], out_dtype=F32, tm=512, tn=512, name=f"s_up{i}")
        cbuf = jnp.transpose(state_conv[i], (1, 0, 2)).reshape((CONV_W - 1) * bd, -1)
        g = _conv_gate_sample(u, cbuf, conv_w[i], row(conv_b[i]), bd=bd, tn=512, name=f"s_gate{i}")
        f = _mm([g], [(lw["down"], g.shape[1], 0)], out_dtype=F32, tm=512, tn=512, name=f"s_down{i}")
        pg = _mm([x1b], [(lw["pg"], d_model, 0)], out_dtype=F32, tm=512, tn=512, name=f"s_pg{i}")
        pe = _mm([ps[i]], [(lw["pe"], ps.shape[-1], 0)], out_dtype=F32, tm=512, tn=512, name=f"s_pe{i}")
        xs, xsb = _ln2(x1, f, pe, pg, row(ln2_g[i]), row(ln2_b[i]), alpha=alpha, tm=256, name=f"s_ln2{i}")
        outs_s[0].append(ckv_s)
        outs_s[1].append(kpe_s)
        outs_s[2].append(ret_new)
        outs_s[3].append(jnp.transpose(ext[t_new:], (1, 0, 2)))
        ext_c = jnp.concatenate([cbuf.reshape(CONV_W - 1, bd, -1), u.reshape(t_new, bd, -1)], 0)
        outs_s[4].append(jnp.transpose(ext_c[t_new:], (1, 0, 2)))

    ckv_p, kpe_p, ret_p, pool_p, conv_p = [jnp.stack(v, 0) for v in outs_p]
    ckv_s, kpe_s, ret_s, pool_s, conv_s = [jnp.stack(v, 0) for v in outs_s]
    y_p = xp.reshape(b, s, d_model)
    y_s = jnp.transpose(xs.reshape(t_new, bd, d_model), (1, 0, 2))
    return (y_p, y_s, ckv_p, kpe_p, ckv_s, kpe_s, ret_p, ret_s, pool_p, pool_s, conv_p, conv_s)
```
